```python
import math, functools
import jax, jax.numpy as jnp
from jax import lax
import numpy as np

D_MODEL = 1024
BATCH = 16
SEQ = 256
DEPTH = 4
DEC_BATCH = 2
DEC_SEQ = 1024
PAST_LEN = 512

GRID_W = 64
N_EVEN = (DEPTH + 1) // 2
N_ODD = DEPTH // 2
D_MIX = D_MODEL
FFN_DIM = 2816
N_MOD = 9
NORM_EPS = 1e-6
ROPE_THETA = 10000.0
Q_BLOCK = 128
FORGET_BIAS = 3.0
HA = 8
DH_A = 64
WIN_R = 8
WIN_C = 16
HB = 8
NOPE_B = 64
ROPE_B = 32
VDIM_B = 64
Q_RANK = 768
KV_RANK = 256
HC = 4
DQK_C = 64
DV_C = 128
CHUNK = 64
HD = 8
HKV_D = 2
DH_D = 64
E_IN = 3 * HA * DH_A + Q_RANK + KV_RANK + ROPE_B
O_IN = 2 * HC * DQK_C + 2 * HC * DV_C + 4 * HC + HD * DH_D + 2 * HKV_D * DH_D

kernel_name = 'hybrid_diffusion_prefix_step'


def rms_norm(x, g):
    xf = x.astype(jnp.float32)
    y = xf * lax.rsqrt(jnp.mean(xf * xf, axis=-1, keepdims=True) + NORM_EPS)
    return (y * g.astype(jnp.float32)).astype(x.dtype)


def split_cols(p, sizes):
    cuts = [int(s) for s in np.cumsum(sizes)[:-1]]
    return jnp.split(p, cuts, axis=-1)


def modulation(cond, w, b):
    return (jax.nn.silu(cond) @ w + b)[:, None, :]


def swiglu(h, w_in, w_out):
    g, u = jnp.split(h @ w_in, 2, axis=-1)
    return (jax.nn.silu(g) * u) @ w_out


def _rotate(x, pos):
    half = x.shape[-1] // 2
    freqs = ROPE_THETA ** (-jnp.arange(half, dtype=jnp.float32) / half)
    ang = pos.astype(jnp.float32)[:, None] * freqs[None, :]
    cos = jnp.cos(ang)[None, :, None, :]
    sin = jnp.sin(ang)[None, :, None, :]
    xf = x.astype(jnp.float32)
    x1, x2 = xf[..., :half], xf[..., half:]
    return jnp.concatenate([x1 * cos - x2 * sin, x2 * cos + x1 * sin], axis=-1).astype(x.dtype)


def axial_rope(x):
    t = jnp.arange(x.shape[1])
    d = x.shape[-1] // 2
    return jnp.concatenate([_rotate(x[..., :d], t // GRID_W), _rotate(x[..., d:], t % GRID_W)], axis=-1)


def block_attention(q, k, v):
    B, T, H, dk = q.shape
    Hk = k.shape[2]
    G = H // Hk
    dv = v.shape[-1]
    scale = dk ** -0.5
    nb = T // Q_BLOCK
    qb = q.reshape(B, nb, Q_BLOCK, Hk, G, dk).swapaxes(0, 1)

    def one_block(q_blk):
        s = jnp.einsum('bqngd,bsnd->bngqs', q_blk, k).astype(jnp.float32) * scale
        p = jax.nn.softmax(s, axis=-1).astype(v.dtype)
        return jnp.einsum('bngqs,bsnd->bqngd', p, v)

    o = lax.map(one_block, qb)
    return o.swapaxes(0, 1).reshape(B, T, H, dv)


def neighbourhood_attention(q, k, v, k_ctx, v_ctx, rpb):
    B, T, H, dh = q.shape
    rows = T // GRID_W
    wr = min(WIN_R, rows)
    scale = dh ** -0.5
    qg = q.reshape(B, rows, GRID_W, H, dh).swapaxes(0, 1)
    kg = k.reshape(B, rows, GRID_W, H, dh)
    vg = v.reshape(B, rows, GRID_W, H, dh)
    r = jnp.arange(rows)
    rs = jnp.clip(r - wr // 2, 0, rows - wr)
    col = jnp.arange(GRID_W)
    cs = jnp.clip(col - WIN_C // 2, 0, GRID_W - WIN_C)
    col_ok = (col[None, :] >= cs[:, None]) & (col[None, :] < cs[:, None] + WIN_C)
    co_idx = jnp.clip(col[None, :] - col[:, None], 1 - WIN_C, WIN_C - 1) + (WIN_C - 1)
    ro = jnp.arange(wr)
    n_band = wr * GRID_W

    def row_block(args):
        q_r, r_i, s_i = args
        kb = lax.dynamic_slice_in_dim(kg, s_i, wr, axis=1)
        vb = lax.dynamic_slice_in_dim(vg, s_i, wr, axis=1)
        ro_idx = s_i + ro - r_i + (WIN_R - 1)
        bias = rpb[:, ro_idx[None, :, None], co_idx[:, None, :]]
        s_band = jnp.einsum('bqhd,bwkhd->bhqwk', q_r, kb).astype(jnp.float32) * scale + bias.astype(jnp.float32)
        s_band = jnp.where(col_ok[:, None, :], s_band, -jnp.inf)
        s_ctx = jnp.einsum('bqhd,blhd->bhql', q_r, k_ctx).astype(jnp.float32) * scale
        s = jnp.concatenate([s_band.reshape(B, H, GRID_W, n_band), s_ctx], axis=-1)
        p = jax.nn.softmax(s, axis=-1).astype(v.dtype)
        p_band = p[..., :n_band].reshape(B, H, GRID_W, wr, GRID_W)
        return (jnp.einsum('bhqwk,bwkhd->bqhd', p_band, vb)
                + jnp.einsum('bhql,blhd->bqhd', p[..., n_band:], v_ctx))

    o = lax.map(row_block, (qg, r, rs))
    return o.swapaxes(0, 1).reshape(B, T, H, dh)


def mlstm_chunked(q, k, v, ig, lf, C0, n0, m0):
    f32 = jnp.float32
    B, T, H, dk = q.shape
    dv = v.shape[-1]
    nc = T // CHUNK

    def chunks(a):
        return a.astype(f32).reshape((B, nc, CHUNK) + a.shape[2:]).swapaxes(0, 1)

    causal = jnp.tril(jnp.ones((CHUNK, CHUNK), dtype=bool))

    def step(carry, inp):
        C, n, m = carry
        qc, kc, vc, ic, fc = inp
        b = jnp.cumsum(fc, axis=1).swapaxes(1, 2)
        it = ic.swapaxes(1, 2)
        log_d = jnp.where(causal, b[..., :, None] - b[..., None, :] + it[..., None, :], -jnp.inf)
        inter = b + m[..., None]
        m_t = jnp.maximum(inter, jnp.max(log_d, axis=-1))
        s = jnp.einsum('blhd,bshd->bhls', qc, kc) * jnp.exp(log_d - m_t[..., None])
        w0 = jnp.exp(inter - m_t)
        num = jnp.einsum('bhls,bshv->bhlv', s, vc) + w0[..., None] * jnp.einsum('blhd,bhvd->bhlv', qc, C)
        den = jnp.sum(s, axis=-1) + w0 * jnp.einsum('blhd,bhd->bhl', qc, n)
        h = num / jnp.maximum(jnp.abs(den), jnp.exp(-m_t))[..., None]
        b_last = b[..., -1]
        g = b_last[..., None] - b + it
        m_new = jnp.maximum(b_last + m, jnp.max(g, axis=-1))
        wg = jnp.exp(g - m_new[..., None])
        decay = jnp.exp(b_last + m - m_new)
        C_new = decay[..., None, None] * C + jnp.einsum('bhs,bshv,bshd->bhvd', wg, vc, kc)
        n_new = decay[..., None] * n + jnp.einsum('bhs,bshd->bhd', wg, kc)
        return (C_new, n_new, m_new), h.swapaxes(1, 2)

    xs = (chunks(q), chunks(k), chunks(v), chunks(ig), chunks(lf))
    (C, n, m), hs = lax.scan(step, (C0.astype(f32), n0.astype(f32), m0.astype(f32)), xs)
    return hs.swapaxes(0, 1).reshape(B, T, H, dv), C, n, m


def mlstm_bidir(q, k, v, gates, gate_bias, C0, n0, m0):
    g = gates.astype(jnp.float32) + gate_bias.astype(jnp.float32)
    flip = lambda a: jnp.flip(a, axis=1)
    h_f, C_f, n_f, m_f = mlstm_chunked(q, k, v, g[:, :, 0], jax.nn.log_sigmoid(g[:, :, 1]),
                                       C0[:, 0], n0[:, 0], m0[:, 0])
    h_b, C_b, n_b, m_b = mlstm_chunked(flip(q), flip(k), flip(v), flip(g[:, :, 2]),
                                       flip(jax.nn.log_sigmoid(g[:, :, 3])), C0[:, 1], n0[:, 1], m0[:, 1])
    return h_f + flip(h_b), jnp.stack([C_f, C_b], axis=1), jnp.stack([n_f, n_b], axis=1), jnp.stack([m_f, m_b], axis=1)


def mla_keys(k_nope, k_rope):
    B, S = k_nope.shape[0], k_nope.shape[1]
    return jnp.concatenate([k_nope, jnp.broadcast_to(k_rope, (B, S, HB, ROPE_B)).astype(k_nope.dtype)], axis=-1)


def even_mixer(h, w_in, w_out, rpb, q_norm, wq_up, kv_norm, wkv_up, ctx):
    B, T, _ = h.shape
    aq, ak, av, cq, ckv, krope = split_cols(h @ w_in, [HA * DH_A] * 3 + [Q_RANK, KV_RANK, ROPE_B])
    aq, ak, av = [t.reshape(B, T, HA, DH_A) for t in (aq, ak, av)]
    qb = (rms_norm(cq, q_norm) @ wq_up).reshape(B, T, HB, NOPE_B + ROPE_B)
    ckv = rms_norm(ckv, kv_norm)
    kvb = (ckv @ wkv_up).reshape(B, T, HB, NOPE_B + VDIM_B)
    if ctx is None:
        oa = block_attention(aq, ak, av)
        kb = mla_keys(kvb[..., :NOPE_B], krope[:, :, None, :])
        ob = block_attention(qb, kb, kvb[..., NOPE_B:])
        new = (ak, av, ckv, krope)
    else:
        c_ak, c_av, c_ckv, c_krope = ctx
        L = c_ckv.shape[1]
        oa = neighbourhood_attention(aq, ak, av, c_ak, c_av, rpb)
        qb = jnp.concatenate([qb[..., :NOPE_B], axial_rope(qb[..., NOPE_B:])], axis=-1)
        kb_lat = mla_keys(kvb[..., :NOPE_B], axial_rope(krope[:, :, None, :]))
        c_kv = (c_ckv @ wkv_up).reshape(B, L, HB, NOPE_B + VDIM_B)
        kb_ctx = mla_keys(c_kv[..., :NOPE_B], c_krope[:, :, None, :])
        ob = block_attention(qb, jnp.concatenate([kb_lat, kb_ctx], axis=1),
                             jnp.concatenate([kvb[..., NOPE_B:], c_kv[..., NOPE_B:]], axis=1))
        new = None
    out = jnp.concatenate([oa.reshape(B, T, HA * DH_A), ob.reshape(B, T, HB * VDIM_B)], axis=-1) @ w_out
    return out, new


def odd_mixer(h, w_in, w_out, gate_bias, out_norm, q_norm, k_norm, ctx):
    B, T, _ = h.shape
    cq, ck, cv, co, cg, dq, dk, dv = split_cols(
        h @ w_in, [HC * DQK_C, HC * DQK_C, HC * DV_C, HC * DV_C, 4 * HC, HD * DH_D, HKV_D * DH_D, HKV_D * DH_D])
    cq = cq.reshape(B, T, HC, DQK_C)
    ck = ck.reshape(B, T, HC, DQK_C) * (DQK_C ** -0.5)
    cv = cv.reshape(B, T, HC, DV_C)
    cg = cg.reshape(B, T, 4, HC)
    dq = rms_norm(dq.reshape(B, T, HD, DH_D), q_norm)
    dk = rms_norm(dk.reshape(B, T, HKV_D, DH_D), k_norm)
    dv = dv.reshape(B, T, HKV_D, DH_D)
    if ctx is None:
        C0 = jnp.zeros((B, 2, HC, DV_C, DQK_C), jnp.float32)
        n0 = jnp.zeros((B, 2, HC, DQK_C), jnp.float32)
        m0 = jnp.zeros((B, 2, HC), jnp.float32)
        od = block_attention(dq, dk, dv)
    else:
        c_dk, c_dv, C0, n0, m0 = ctx
        od = block_attention(axial_rope(dq), jnp.concatenate([axial_rope(dk), c_dk], axis=1),
                             jnp.concatenate([dv, c_dv], axis=1))
    hc, C, n, m = mlstm_bidir(cq, ck, cv, cg, gate_bias, C0, n0, m0)
    hc = (jax.nn.sigmoid(co.reshape(B, T, HC, DV_C).astype(jnp.float32)) * rms_norm(hc, out_norm)).astype(h.dtype)
    out = jnp.concatenate([hc.reshape(B, T, HC * DV_C), od.reshape(B, T, HD * DH_D)], axis=-1) @ w_out
    new = (dk, dv, C, n, m) if ctx is None else None
    return out, new


def layer(x, mod, norm_g, ffn_in, ffn_out, mixer):
    sh1, sc1, g1, sh2, sc2, g2, sh3, sc3, g3 = jnp.split(mod, N_MOD, axis=-1)
    x = x + 0.5 * g1 * swiglu(rms_norm(x, norm_g[0]) * (1 + sc1) + sh1, ffn_in[0], ffn_out[0])
    mix, new = mixer(rms_norm(x, norm_g[1]) * (1 + sc2) + sh2)
    x = x + g2 * mix
    x = x + 0.5 * g3 * swiglu(rms_norm(x, norm_g[2]) * (1 + sc3) + sh3, ffn_in[1], ffn_out[1])
    return x, new


def setup_inputs(seed: int = 0) -> dict:
    key = jax.random.key(seed)
    ks = iter(jax.random.split(key, 40))

    def nrm(shape, s=1.0):
        return s * jax.random.normal(next(ks), shape, jnp.float32)

    D = D_MODEL
    gate_center = jnp.array([0.0, FORGET_BIAS, 0.0, FORGET_BIAS], jnp.float32)[None, :, None]
    return {
        'x_prompt': nrm((BATCH, SEQ, D)),
        'x_sample': nrm((DEC_BATCH, DEC_SEQ, D)),
        'cache_a_k': nrm((DEC_BATCH, N_EVEN, PAST_LEN, HA, DH_A)),
        'cache_a_v': nrm((DEC_BATCH, N_EVEN, PAST_LEN, HA, DH_A)),
        'cache_b_ckv': nrm((DEC_BATCH, N_EVEN, PAST_LEN, KV_RANK)),
        'cache_b_krope': nrm((DEC_BATCH, N_EVEN, PAST_LEN, ROPE_B)),
        'cache_d_k': nrm((DEC_BATCH, N_ODD, PAST_LEN, HKV_D, DH_D)),
        'cache_d_v': nrm((DEC_BATCH, N_ODD, PAST_LEN, HKV_D, DH_D)),
        'state_c_C': nrm((DEC_BATCH, N_ODD, 2, HC, DV_C, DQK_C), 0.1),
        'state_c_n': nrm((DEC_BATCH, N_ODD, 2, HC, DQK_C), 0.1),
        'state_c_m': nrm((DEC_BATCH, N_ODD, 2, HC), 0.5),
        'c': nrm((DEC_BATCH, D)),
        'c_ctx': nrm((D,)),
        'w_mod': nrm((DEPTH, D, N_MOD * D), 0.5 * D ** -0.5),
        'b_mod': nrm((DEPTH, N_MOD * D), 0.01),
        'norm_g': 1.0 + nrm((DEPTH, 3, D), 0.02),
        'ffn_in': nrm((DEPTH, 2, D, 2 * FFN_DIM), D ** -0.5),
        'ffn_out': nrm((DEPTH, 2, FFN_DIM, D), FFN_DIM ** -0.5),
        'w_in_even': nrm((N_EVEN, D, E_IN), D ** -0.5),
        'w_in_odd': nrm((N_ODD, D, O_IN), D ** -0.5),
        'w_out': nrm((DEPTH, D_MIX, D), D_MIX ** -0.5),
        'a_rpb': nrm((N_EVEN, HA, 2 * WIN_R - 1, 2 * WIN_C - 1), 0.1),
        'b_q_norm': 1.0 + nrm((N_EVEN, Q_RANK), 0.02),
        'b_wq_up': nrm((N_EVEN, Q_RANK, HB * (NOPE_B + ROPE_B)), Q_RANK ** -0.5),
        'b_kv_norm': 1.0 + nrm((N_EVEN, KV_RANK), 0.02),
        'b_wkv_up': nrm((N_EVEN, KV_RANK, HB * (NOPE_B + VDIM_B)), KV_RANK ** -0.5),
        'c_gate_bias': gate_center + nrm((N_ODD, 4, HC), 0.1),
        'c_out_norm': 1.0 + nrm((N_ODD, HC, DV_C), 0.02),
        'd_q_norm': 1.0 + nrm((N_ODD, DH_D), 0.02),
        'd_k_norm': 1.0 + nrm((N_ODD, DH_D), 0.02),
        'final_norm': 1.0 + nrm((D,), 0.02),
    }


def reference(x_prompt, x_sample, cache_a_k, cache_a_v, cache_b_ckv, cache_b_krope, cache_d_k, cache_d_v,
              state_c_C, state_c_n, state_c_m, c, c_ctx, w_mod, b_mod, norm_g, ffn_in, ffn_out,
              w_in_even, w_in_odd, w_out, a_rpb, b_q_norm, b_wq_up, b_kv_norm, b_wkv_up,
              c_gate_bias, c_out_norm, d_q_norm, d_k_norm, final_norm):
    xp = x_prompt
    xs = x_sample
    even_new = []
    odd_new = []
    for l in range(DEPTH):
        m_ctx = modulation(c_ctx[None, :], w_mod[l], b_mod[l])
        m_lat = modulation(c, w_mod[l], b_mod[l])
        if l % 2 == 0:
            e = l // 2
            mixer = functools.partial(even_mixer, w_in=w_in_even[e], w_out=w_out[l], rpb=a_rpb[e],
                                      q_norm=b_q_norm[e], wq_up=b_wq_up[e], kv_norm=b_kv_norm[e],
                                      wkv_up=b_wkv_up[e])
            ctx_cache = (cache_a_k[:, e], cache_a_v[:, e], cache_b_ckv[:, e], cache_b_krope[:, e])
            xp, new = layer(xp, m_ctx, norm_g[l], ffn_in[l], ffn_out[l], functools.partial(mixer, ctx=None))
            xs, _ = layer(xs, m_lat, norm_g[l], ffn_in[l], ffn_out[l], functools.partial(mixer, ctx=ctx_cache))
            even_new.append(new)
        else:
            o = l // 2
            mixer = functools.partial(odd_mixer, w_in=w_in_odd[o], w_out=w_out[l], gate_bias=c_gate_bias[o],
                                      out_norm=c_out_norm[o], q_norm=d_q_norm[o], k_norm=d_k_norm[o])
            ctx_cache = (cache_d_k[:, o], cache_d_v[:, o], state_c_C[:, o], state_c_n[:, o], state_c_m[:, o])
            xp, new = layer(xp, m_ctx, norm_g[l], ffn_in[l], ffn_out[l], functools.partial(mixer, ctx=None))
            xs, _ = layer(xs, m_lat, norm_g[l], ffn_in[l], ffn_out[l], functools.partial(mixer, ctx=ctx_cache))
            odd_new.append(new)
    y_prompt = rms_norm(xp, final_norm)
    y_sample = rms_norm(xs, final_norm)
    new_a_k = jnp.stack([t[0] for t in even_new], axis=1)
    new_a_v = jnp.stack([t[1] for t in even_new], axis=1)
    new_b_ckv = jnp.stack([t[2] for t in even_new], axis=1)
    new_b_krope = jnp.stack([t[3] for t in even_new], axis=1)
    new_d_k = jnp.stack([t[0] for t in odd_new], axis=1)
    new_d_v = jnp.stack([t[1] for t in odd_new], axis=1)
    new_c_C = jnp.stack([t[2] for t in odd_new], axis=1)
    new_c_n = jnp.stack([t[3] for t in odd_new], axis=1)
    new_c_m = jnp.stack([t[4] for t in odd_new], axis=1)
    return (y_prompt, y_sample, new_a_k, new_a_v, new_b_ckv, new_b_krope, new_d_k, new_d_v, new_c_C, new_c_n, new_c_m)
```

```python
import functools

import numpy as np
import jax
import jax.numpy as jnp
from jax import lax
from jax.experimental import pallas as pl
from jax.experimental.pallas import tpu as pltpu

F32 = jnp.float32
BF16 = jnp.bfloat16

D_MODEL = 1024
BATCH = 16
SEQ = 256
DEPTH = 4
DEC_BATCH = 2
DEC_SEQ = 1024
PAST_LEN = 512
GRID_W = 64
GRID_ROWS = DEC_SEQ // GRID_W
N_EVEN = (DEPTH + 1) // 2
N_ODD = DEPTH // 2
FFN_DIM = 2816
N_MOD = 9
NORM_EPS = 1e-6
ROPE_THETA = 10000.0
HA, DH_A, WIN_R, WIN_C = 8, 64, 8, 16
HB, NOPE_B, ROPE_B, VDIM_B, Q_RANK, KV_RANK = 8, 64, 32, 64, 768, 256
HC, DQK_C, DV_C, CHUNK = 4, 64, 128, 64
HD, HKV_D, DH_D = 8, 2, 64
E_IN = 3 * HA * DH_A + Q_RANK + KV_RANK + ROPE_B
O_IN = 2 * HC * DQK_C + 2 * HC * DV_C + 4 * HC + HD * DH_D + 2 * HKV_D * DH_D

N_COND = 8
MIB = 1024 * 1024

FFN_TM = 1024
FFN_TF = 256
PROJ_TM = 256
OPROJ_TM = 512
ATTN_TQ = 256


def _params(semantics, vmem_mib):
    return pltpu.CompilerParams(dimension_semantics=semantics, vmem_limit_bytes=vmem_mib * MIB)


def _dot(a, b):
    return jnp.dot(a, b, preferred_element_type=F32)


def _dot_nt(a, b):
    return lax.dot_general(a, b, (((1,), (1,)), ((), ())), preferred_element_type=F32)


def _rms(x):
    return x * lax.rsqrt(jnp.mean(x * x, axis=-1, keepdims=True) + NORM_EPS)


def _norm_mod(x, g, scale, shift):
    return (_rms(x) * g) * (1.0 + scale) + shift


def _log_sigmoid(x):
    return jnp.minimum(x, 0.0) - jnp.log1p(jnp.exp(-jnp.abs(x)))


def _split3(x):
    hi = x.astype(BF16)
    r = x - hi.astype(F32)
    mid = r.astype(BF16)
    lo = (r - mid.astype(F32)).astype(BF16)
    return hi, mid, lo


def _head_inv_rms(x, width):
    n = x.shape[1]
    ri = lax.broadcasted_iota(jnp.int32, (n, n), 0) // width
    ci = lax.broadcasted_iota(jnp.int32, (n, n), 1) // width
    same = (ri == ci).astype(BF16)
    hi, mid, lo = _split3(x * x)
    ss = _dot(hi, same) + _dot(mid, same) + _dot(lo, same)
    return lax.rsqrt(ss * (1.0 / width) + NORM_EPS)


def _softmax_pv(scores, values):
    m = jnp.max(scores[0], axis=-1, keepdims=True)
    for s in scores[1:]:
        m = jnp.maximum(m, jnp.max(s, axis=-1, keepdims=True))
    l = None
    o = None
    for s, v in zip(scores, values):
        p = jnp.exp(s - m)
        ls = jnp.sum(p, axis=-1, keepdims=True)
        os_ = _dot(p.astype(BF16), v)
        l = ls if l is None else l + ls
        o = os_ if o is None else o + os_
    return o / l


def _mod_kernel(c_ref, w_ref, b_ref, o_ref):
    c = c_ref[...]
    s = (c * jax.nn.sigmoid(c)).astype(BF16)
    o_ref[...] = _dot(s, w_ref[...].astype(BF16)) + b_ref[...]


def _modulation(cond, w_mod, b_mod):
    tn = 1024
    out = pl.pallas_call(
        _mod_kernel,
        grid=(DEPTH, N_MOD * D_MODEL // tn),
        in_specs=[pl.BlockSpec((N_COND, D_MODEL), lambda l, j: (0, 0)),
                  pl.BlockSpec((None, D_MODEL, tn), lambda l, j: (l, 0, j)),
                  pl.BlockSpec((None, 1, tn), lambda l, j: (l, 0, j))],
        out_specs=pl.BlockSpec((None, N_COND, tn), lambda l, j: (l, 0, j)),
        out_shape=jax.ShapeDtypeStruct((DEPTH, N_COND, N_MOD * D_MODEL), F32),
        compiler_params=_params(("arbitrary", "arbitrary"), 32),
        name="modulation",
    )(cond, w_mod, b_mod.reshape(DEPTH, 1, N_MOD * D_MODEL))
    return out.reshape(DEPTH, N_COND, N_MOD, D_MODEL)


def _ffn_kernel(x_ref, mod_ref, g_ref, wg_ref, wu_ref, wo_ref, fin_ref, o_ref, h_ref, *, k, nf, final):
    j = pl.program_id(1)

    @pl.when(j == 0)
    def _():
        h = _norm_mod(x_ref[...], g_ref[...], mod_ref[3 * k + 1:3 * k + 2, :], mod_ref[3 * k:3 * k + 1, :])
        h_ref[...] = h.astype(BF16)

    h = h_ref[...]
    a = _dot(h, wg_ref[...].astype(BF16))
    u = _dot(h, wu_ref[...].astype(BF16))
    act = ((a * jax.nn.sigmoid(a)) * u).astype(BF16)
    part = _dot(act, wo_ref[...].astype(BF16))

    @pl.when(j == 0)
    def _():
        o_ref[...] = part

    @pl.when(j > 0)
    def _():
        o_ref[...] += part

    @pl.when(j == nf - 1)
    def _():
        y = x_ref[...] + (0.5 * mod_ref[3 * k + 2:3 * k + 3, :]) * o_ref[...]
        if final:
            y = _rms(y) * fin_ref[...]
        o_ref[...] = y


def _ffn(x, mod, norm_g, ffn_in, ffn_out, final_norm, *, l, k, group_of_tile, final):
    R = x.shape[0]
    tm, tf = FFN_TM, FFN_TF
    nf = FFN_DIM // tf
    w = 0 if k == 0 else 1
    return pl.pallas_call(
        functools.partial(_ffn_kernel, k=k, nf=nf, final=final),
        grid=(R // tm, nf),
        in_specs=[pl.BlockSpec((tm, D_MODEL), lambda i, j: (i, 0)),
                  pl.BlockSpec((None, None, N_MOD, D_MODEL), lambda i, j: (l, group_of_tile(i), 0, 0)),
                  pl.BlockSpec((None, None, 1, D_MODEL), lambda i, j: (l, k, 0, 0)),
                  pl.BlockSpec((None, None, D_MODEL, tf), lambda i, j: (l, w, 0, j)),
                  pl.BlockSpec((None, None, D_MODEL, tf), lambda i, j: (l, w, 0, j + nf)),
                  pl.BlockSpec((None, None, tf, D_MODEL), lambda i, j: (l, w, j, 0)),
                  pl.BlockSpec((1, D_MODEL), lambda i, j: (0, 0))],
        out_specs=pl.BlockSpec((tm, D_MODEL), lambda i, j: (i, 0)),
        out_shape=jax.ShapeDtypeStruct((R, D_MODEL), F32),
        scratch_shapes=[pltpu.VMEM((tm, D_MODEL), BF16)],
        compiler_params=_params(("arbitrary", "arbitrary"), 48),
        name="ffn",
    )(x, mod, norm_g.reshape(DEPTH, 3, 1, D_MODEL), ffn_in, ffn_in, ffn_out, final_norm.reshape(1, D_MODEL))


def _oproj_kernel(*refs, odd):
    if odd:
        x_ref, mod_ref, hf_ref, hb_ref, co_ref, on_ref, r_ref, w_ref, o_ref, w_s = refs
    else:
        x_ref, mod_ref, l_ref, r_ref, w_ref, o_ref, w_s = refs

    @pl.when(pl.program_id(0) == 0)
    def _():
        w_s[...] = w_ref[...].astype(BF16)

    if odd:
        hc = hf_ref[...] + hb_ref[...]
        normed = jnp.concatenate([_rms(hc[:, i * DV_C:(i + 1) * DV_C]) for i in range(HC)], axis=-1)
        left = (jax.nn.sigmoid(co_ref[...]) * (normed * on_ref[...])).astype(BF16)
    else:
        left = l_ref[...].astype(BF16)
    right = r_ref[...].astype(BF16)
    half = D_MODEL // 2
    mix = _dot(left, w_s[0:half, :]) + _dot(right, w_s[half:D_MODEL, :])
    o_ref[...] = x_ref[...] + mod_ref[5:6, :] * mix


def _oproj(x, mod, w_out, left_inputs, right, *, l, group_of_tile, odd, out_norm=None):
    R = x.shape[0]
    tm = OPROJ_TM
    half = D_MODEL // 2
    row = lambda i: (i, 0)
    specs = [pl.BlockSpec((tm, D_MODEL), row),
             pl.BlockSpec((None, None, N_MOD, D_MODEL), lambda i: (l, group_of_tile(i), 0, 0))]
    args = [x, mod]
    if odd:
        hf, hb, co = left_inputs
        specs += [pl.BlockSpec((tm, half), row)] * 3 + [pl.BlockSpec((1, half), lambda i: (0, 0))]
        args += [hf, hb, co, out_norm.reshape(1, half)]
    else:
        specs += [pl.BlockSpec((tm, half), row)]
        args += [left_inputs]
    specs += [pl.BlockSpec((tm, half), row), pl.BlockSpec((None, D_MODEL, D_MODEL), lambda i: (l, 0, 0))]
    args += [right, w_out]
    return pl.pallas_call(
        functools.partial(_oproj_kernel, odd=odd),
        grid=(R // tm,),
        in_specs=specs,
        out_specs=pl.BlockSpec((tm, D_MODEL), row),
        out_shape=jax.ShapeDtypeStruct((R, D_MODEL), F32),
        scratch_shapes=[pltpu.VMEM((D_MODEL, D_MODEL), BF16)],
        compiler_params=_params(("arbitrary",), 40),
        name="oproj_odd" if odd else "oproj_even",
    )(*args)


def _proj_even_kernel(*refs, rope):
    if rope:
        (x_ref, mod_ref, g_ref, win_ref, qn_ref, wq_ref, kvn_ref, wkv_ref, wsw_ref, cq_ref, sq_ref, ck_ref, sk_ref,
         aq_o, ak_o, av_o, qnope_o, qrope_o, ckv_o, kr_o, kn_o, vb_o, win_s, wq_s, wkv_s, wsw_s) = refs
    else:
        (x_ref, mod_ref, g_ref, win_ref, qn_ref, wq_ref, kvn_ref, wkv_ref,
         aq_o, ak_o, av_o, qnope_o, qrope_o, ckv_o, kr_o, kn_o, vb_o, win_s, wq_s, wkv_s) = refs

    @pl.when(pl.program_id(0) == 0)
    def _():
        win_s[...] = win_ref[...].astype(BF16)
        wq_s[...] = wq_ref[...].astype(BF16)
        wkv_s[...] = wkv_ref[...].astype(BF16)
        if rope:
            wsw_s[...] = wsw_ref[...].astype(BF16)

    h = _norm_mod(x_ref[...], g_ref[...], mod_ref[4:5, :], mod_ref[3:4, :]).astype(BF16)
    na = HA * DH_A
    aq_o[...] = _dot(h, win_s[:, 0:na]).astype(aq_o.dtype)
    ak_o[...] = _dot(h, win_s[:, na:2 * na]).astype(ak_o.dtype)
    av_o[...] = _dot(h, win_s[:, 2 * na:3 * na]).astype(av_o.dtype)
    c0 = 3 * na
    cq = _dot(h, win_s[:, c0:c0 + Q_RANK])
    cqn = (_rms(cq) * qn_ref[...]).astype(BF16)
    qb = _dot(cqn, wq_s[...])
    nn = HB * NOPE_B
    nr = HB * ROPE_B
    qnope_o[...] = qb[:, 0:nn].astype(qnope_o.dtype)
    if rope:
        qr = qb[:, nn:nn + nr] * cq_ref[...] + qb[:, nn + nr:nn + 2 * nr] * sq_ref[...]
    else:
        qr = qb[:, nn:nn + nr]
    qrope_o[...] = qr.astype(qrope_o.dtype)
    c1 = c0 + Q_RANK
    ckv = _dot(h, win_s[:, c1:c1 + KV_RANK])
    ckvn = _rms(ckv) * kvn_ref[...]
    ckv_o[...] = ckvn.astype(ckv_o.dtype)
    kvb = _dot(ckvn.astype(BF16), wkv_s[...])
    kn_o[...] = kvb[:, 0:nn].astype(kn_o.dtype)
    vb_o[...] = kvb[:, nn:nn + HB * VDIM_B].astype(vb_o.dtype)
    c2 = c1 + KV_RANK
    kr = _dot(h, win_s[:, c2:c2 + ROPE_B])
    if rope:
        kr = kr * ck_ref[...] + _dot(h, wsw_s[...]) * sk_ref[...]
    kr_o[...] = kr.astype(kr_o.dtype)


def _proj_even(x, mod, norm_g, w_in, q_norm, wq_p, kv_norm, wkv_p, *, l, e, group_of_tile, rope_args=None):
    R = x.shape[0]
    tm = PROJ_TM
    rope = rope_args is not None
    row = lambda i: (i, 0)
    const2 = lambda i: (0, 0)
    nq = HB * (NOPE_B + ROPE_B) + (HB * ROPE_B if rope else 0)
    specs = [pl.BlockSpec((tm, D_MODEL), row),
             pl.BlockSpec((None, None, N_MOD, D_MODEL), lambda i: (l, group_of_tile(i), 0, 0)),
             pl.BlockSpec((None, None, 1, D_MODEL), lambda i: (l, 1, 0, 0)),
             pl.BlockSpec((None, D_MODEL, E_IN), lambda i: (e, 0, 0)),
             pl.BlockSpec((None, 1, Q_RANK), lambda i: (e, 0, 0)),
             pl.BlockSpec((None, Q_RANK, nq), lambda i: (e, 0, 0)),
             pl.BlockSpec((None, 1, KV_RANK), lambda i: (e, 0, 0)),
             pl.BlockSpec((None, KV_RANK, HB * (NOPE_B + VDIM_B)), lambda i: (e, 0, 0))]
    args = [x, mod, norm_g.reshape(DEPTH, 3, 1, D_MODEL), w_in, q_norm.reshape(N_EVEN, 1, Q_RANK), wq_p,
            kv_norm.reshape(N_EVEN, 1, KV_RANK), wkv_p]
    scratch = [pltpu.VMEM((D_MODEL, E_IN), BF16), pltpu.VMEM((Q_RANK, nq), BF16),
               pltpu.VMEM((KV_RANK, HB * (NOPE_B + VDIM_B)), BF16)]
    if rope:
        w_sw, cos_q, sin_q, cos_k, sin_k = rope_args
        nt = DEC_SEQ // tm
        pos = lambda i: (i % nt, 0)
        specs += [pl.BlockSpec((None, D_MODEL, ROPE_B), lambda i: (e, 0, 0)),
                  pl.BlockSpec((tm, HB * ROPE_B), pos), pl.BlockSpec((tm, HB * ROPE_B), pos),
                  pl.BlockSpec((tm, ROPE_B), pos), pl.BlockSpec((tm, ROPE_B), pos)]
        args += [w_sw, cos_q, sin_q, cos_k, sin_k]
        scratch += [pltpu.VMEM((D_MODEL, ROPE_B), BF16)]
        cache_dt = BF16
    else:
        cache_dt = F32
    widths = [(HA * DH_A, BF16), (HA * DH_A, cache_dt), (HA * DH_A, cache_dt), (HB * NOPE_B, BF16),
              (HB * ROPE_B, BF16), (KV_RANK, cache_dt), (ROPE_B, cache_dt), (HB * NOPE_B, BF16), (HB * VDIM_B, BF16)]
    return pl.pallas_call(
        functools.partial(_proj_even_kernel, rope=rope),
        grid=(R // tm,),
        in_specs=specs,
        out_specs=[pl.BlockSpec((tm, wd), row) for wd, _ in widths],
        out_shape=[jax.ShapeDtypeStruct((R, wd), dt) for wd, dt in widths],
        scratch_shapes=scratch,
        compiler_params=_params(("arbitrary",), 56),
        name="proj_even_rope" if rope else "proj_even",
    )(*args)


SCALE_A = DH_A ** -0.5
SCALE_B = (NOPE_B + ROPE_B) ** -0.5
SCALE_D = DH_D ** -0.5


def _attn_even_prompt_kernel(aq, ak, av, qn, qr, kn, kr, vb, oa_ref, ob_ref):
    krb = kr[...].astype(BF16)
    for h in range(HA):
        sl = slice(h * DH_A, (h + 1) * DH_A)
        s = _dot_nt(aq[:, sl], ak[:, sl].astype(BF16)) * SCALE_A
        oa_ref[:, sl] = _softmax_pv([s], [av[:, sl].astype(BF16)]).astype(oa_ref.dtype)
    for h in range(HB):
        sl = slice(h * NOPE_B, (h + 1) * NOPE_B)
        sr = slice(h * ROPE_B, (h + 1) * ROPE_B)
        s = (_dot_nt(qn[:, sl], kn[:, sl]) + _dot_nt(qr[:, sr], krb)) * SCALE_B
        ob_ref[:, h * VDIM_B:(h + 1) * VDIM_B] = _softmax_pv([s], [vb[:, sl]]).astype(ob_ref.dtype)


def _attn_even_prompt(aq, ak, av, qn, qr, kn, kr, vb):
    R = aq.shape[0]
    row = lambda b: (b, 0)
    ins = [aq, ak, av, qn, qr, kn, kr, vb]
    return pl.pallas_call(
        _attn_even_prompt_kernel,
        grid=(R // SEQ,),
        in_specs=[pl.BlockSpec((SEQ, a.shape[1]), row) for a in ins],
        out_specs=[pl.BlockSpec((SEQ, 512), row)] * 2,
        out_shape=[jax.ShapeDtypeStruct((R, 512), BF16)] * 2,
        compiler_params=_params(("arbitrary",), 32),
        name="attn_even_prompt",
    )(*ins)


def _nbr_kernel(q_ref, k_ref, v_ref, kc_ref, vc_ref, bias_ref, o_ref):
    r = pl.program_id(1)
    rs = jnp.clip(r - WIN_R // 2, 0, GRID_ROWS - WIN_R)
    start = pl.multiple_of(rs * GRID_W, GRID_W)
    nb = WIN_R * GRID_W
    for h in range(HA):
        sl = slice(h * DH_A, (h + 1) * DH_A)
        q = q_ref[:, sl]
        kb = k_ref[pl.ds(start, nb), sl]
        vb = v_ref[pl.ds(start, nb), sl]
        s_band = _dot_nt(q, kb) * SCALE_A + bias_ref[h]
        s_ctx = _dot_nt(q, kc_ref[:, sl].astype(BF16)) * SCALE_A
        o_ref[:, sl] = _softmax_pv([s_band, s_ctx], [vb, vc_ref[:, sl].astype(BF16)]).astype(o_ref.dtype)


def _nbr_attention(aq, ak, av, cache_k, cache_v, bias, *, e):
    nr = GRID_ROWS

    def bias_idx(b, r):
        return (e, 0, r - jnp.clip(r - WIN_R // 2, 0, GRID_ROWS - WIN_R), 0, 0)

    return pl.pallas_call(
        _nbr_kernel,
        grid=(DEC_BATCH, nr),
        in_specs=[pl.BlockSpec((GRID_W, 512), lambda b, r: (b * nr + r, 0)),
                  pl.BlockSpec((DEC_SEQ, 512), lambda b, r: (b, 0)),
                  pl.BlockSpec((DEC_SEQ, 512), lambda b, r: (b, 0)),
                  pl.BlockSpec((None, None, PAST_LEN, 512), lambda b, r: (b, e, 0, 0)),
                  pl.BlockSpec((None, None, PAST_LEN, 512), lambda b, r: (b, e, 0, 0)),
                  pl.BlockSpec((None, HA, None, GRID_W, WIN_R * GRID_W), bias_idx)],
        out_specs=pl.BlockSpec((GRID_W, 512), lambda b, r: (b * nr + r, 0)),
        out_shape=jax.ShapeDtypeStruct((DEC_BATCH * DEC_SEQ, 512), BF16),
        compiler_params=_params(("arbitrary", "arbitrary"), 32),
        name="nbr_attention",
    )(aq, ak, av, cache_k, cache_v, bias)


def _mla_sample_kernel(qn, qr, kn, kr, vb, cckv, ckr, wkv, o_ref, knc_s, vc_s):
    @pl.when(pl.program_id(1) == 0)
    def _():
        ckv = _dot(cckv[...].astype(BF16), wkv[...].astype(BF16))
        nn = HB * NOPE_B
        knc_s[...] = ckv[:, 0:nn].astype(BF16)
        vc_s[...] = ckv[:, nn:nn + HB * VDIM_B].astype(BF16)

    krl = kr[...]
    krc = ckr[...].astype(BF16)
    for h in range(HB):
        sl = slice(h * NOPE_B, (h + 1) * NOPE_B)
        sr = slice(h * ROPE_B, (h + 1) * ROPE_B)
        q1 = qn[:, sl]
        q2 = qr[:, sr]
        s_lat = (_dot_nt(q1, kn[:, sl]) + _dot_nt(q2, krl)) * SCALE_B
        s_ctx = (_dot_nt(q1, knc_s[:, sl]) + _dot_nt(q2, krc)) * SCALE_B
        o_ref[:, h * VDIM_B:(h + 1) * VDIM_B] = _softmax_pv([s_lat, s_ctx], [vb[:, sl], vc_s[:, sl]]).astype(o_ref.dtype)


def _mla_sample(qn, qr, kn, kr, vb, cache_ckv, cache_kr, wkv_p, *, e):
    tq = ATTN_TQ
    nt = DEC_SEQ // tq
    qrow = lambda b, i: (b * nt + i, 0)
    brow = lambda b, i: (b, 0)
    return pl.pallas_call(
        _mla_sample_kernel,
        grid=(DEC_BATCH, nt),
        in_specs=[pl.BlockSpec((tq, 512), qrow), pl.BlockSpec((tq, 256), qrow),
                  pl.BlockSpec((DEC_SEQ, 512), brow), pl.BlockSpec((DEC_SEQ, ROPE_B), brow),
                  pl.BlockSpec((DEC_SEQ, 512), brow),
                  pl.BlockSpec((None, None, PAST_LEN, KV_RANK), lambda b, i: (b, e, 0, 0)),
                  pl.BlockSpec((None, None, PAST_LEN, ROPE_B), lambda b, i: (b, e, 0, 0)),
                  pl.BlockSpec((None, KV_RANK, 1024), lambda b, i: (e, 0, 0))],
        out_specs=pl.BlockSpec((tq, 512), qrow),
        out_shape=jax.ShapeDtypeStruct((DEC_BATCH * DEC_SEQ, 512), BF16),
        scratch_shapes=[pltpu.VMEM((PAST_LEN, 512), BF16), pltpu.VMEM((PAST_LEN, 512), BF16)],
        compiler_params=_params(("arbitrary", "arbitrary"), 32),
        name="mla_sample",
    )(qn, qr, kn, kr, vb, cache_ckv, cache_kr, wkv_p)


def _proj_odd_kernel(*refs, rope):
    if rope:
        (x_ref, mod_ref, g_ref, w1_ref, wg_ref, wd_ref, qn_ref, kn_ref, wsw_ref, qnsw_ref, knsw_ref,
         cq_ref, sq_ref, ck_ref, sk_ref,
         cq_o, ck_o, cv_o, co_o, gt_o, dq_o, dk_o, dv_o, w1_s, wg_s, wd_s, wsw_s) = refs
    else:
        (x_ref, mod_ref, g_ref, w1_ref, wg_ref, wd_ref, qn_ref, kn_ref,
         cq_o, ck_o, cv_o, co_o, gt_o, dq_o, dk_o, dv_o, w1_s, wg_s, wd_s) = refs

    @pl.when(pl.program_id(0) == 0)
    def _():
        w1_s[...] = w1_ref[...].astype(BF16)
        wg_s[...] = wg_ref[...].astype(BF16)
        wd_s[...] = wd_ref[...].astype(BF16)
        if rope:
            wsw_s[...] = wsw_ref[...].astype(BF16)

    h = _norm_mod(x_ref[...], g_ref[...], mod_ref[4:5, :], mod_ref[3:4, :]).astype(BF16)
    nqk = HC * DQK_C
    nv = HC * DV_C
    cq_o[...] = _dot(h, w1_s[:, 0:nqk]).astype(cq_o.dtype)
    ck_o[...] = (_dot(h, w1_s[:, nqk:2 * nqk]) * (DQK_C ** -0.5)).astype(ck_o.dtype)
    cv_o[...] = _dot(h, w1_s[:, 2 * nqk:2 * nqk + nv]).astype(cv_o.dtype)
    co_o[...] = _dot(h, w1_s[:, 2 * nqk + nv:2 * nqk + 2 * nv])
    gt_o[...] = _dot(h, wg_s[...])
    nq = HD * DH_D
    nk = HKV_D * DH_D
    dq = _dot(h, wd_s[:, 0:nq])
    dk = _dot(h, wd_s[:, nq:nq + nk])
    dv_o[...] = _dot(h, wd_s[:, nq + nk:nq + 2 * nk]).astype(dv_o.dtype)
    rq = _head_inv_rms(dq, DH_D)
    rk = _head_inv_rms(dk, DH_D)
    dqn = (dq * rq) * qn_ref[...]
    dkn = (dk * rk) * kn_ref[...]
    if rope:
        dq_sw = _dot(h, wsw_s[:, 0:nq])
        dk_sw = _dot(h, wsw_s[:, nq:nq + nk])
        dqn = dqn * cq_ref[...] + ((dq_sw * rq) * qnsw_ref[...]) * sq_ref[...]
        dkn = dkn * ck_ref[...] + ((dk_sw * rk) * knsw_ref[...]) * sk_ref[...]
    dq_o[...] = dqn.astype(dq_o.dtype)
    dk_o[...] = dkn.astype(dk_o.dtype)


def _proj_odd(x, mod, norm_g, w1, wg, wd, qn_t, kn_t, *, l, o, group_of_tile, rope_args=None):
    R = x.shape[0]
    tm = PROJ_TM
    rope = rope_args is not None
    row = lambda i: (i, 0)
    n1 = 2 * HC * DQK_C + 2 * HC * DV_C
    ng = 4 * HC
    nq = HD * DH_D
    nk = HKV_D * DH_D
    nd = nq + 2 * nk
    wspec = lambda n: pl.BlockSpec((None, D_MODEL, n), lambda i: (o, 0, 0))
    vspec = lambda n: pl.BlockSpec((None, 1, n), lambda i: (o, 0, 0))
    specs = [pl.BlockSpec((tm, D_MODEL), row),
             pl.BlockSpec((None, None, N_MOD, D_MODEL), lambda i: (l, group_of_tile(i), 0, 0)),
             pl.BlockSpec((None, None, 1, D_MODEL), lambda i: (l, 1, 0, 0)),
             wspec(n1), wspec(ng), wspec(nd), vspec(nq), vspec(nk)]
    args = [x, mod, norm_g.reshape(DEPTH, 3, 1, D_MODEL), w1, wg, wd, qn_t, kn_t]
    scratch = [pltpu.VMEM((D_MODEL, n1), BF16), pltpu.VMEM((D_MODEL, ng), BF16), pltpu.VMEM((D_MODEL, nd), BF16)]
    if rope:
        w_sw, qnsw_t, knsw_t, cos_q, sin_q, cos_k, sin_k = rope_args
        nt = DEC_SEQ // tm
        pos = lambda i: (i % nt, 0)
        specs += [wspec(nq + nk), vspec(nq), vspec(nk),
                  pl.BlockSpec((tm, nq), pos), pl.BlockSpec((tm, nq), pos),
                  pl.BlockSpec((tm, nk), pos), pl.BlockSpec((tm, nk), pos)]
        args += [w_sw, qnsw_t, knsw_t, cos_q, sin_q, cos_k, sin_k]
        scratch += [pltpu.VMEM((D_MODEL, nq + nk), BF16)]
        cache_dt = BF16
    else:
        cache_dt = F32
    widths = [(HC * DQK_C, BF16), (HC * DQK_C, BF16), (HC * DV_C, BF16), (HC * DV_C, F32), (ng, F32),
              (nq, BF16), (nk, cache_dt), (nk, cache_dt)]
    return pl.pallas_call(
        functools.partial(_proj_odd_kernel, rope=rope),
        grid=(R // tm,),
        in_specs=specs,
        out_specs=[pl.BlockSpec((tm, wd), row) for wd, _ in widths],
        out_shape=[jax.ShapeDtypeStruct((R, wd), dt) for wd, dt in widths],
        scratch_shapes=scratch,
        compiler_params=_params(("arbitrary",), 56),
        name="proj_odd_rope" if rope else "proj_odd",
    )(*args)


def _gqa_kernel(*refs, ctx):
    if ctx:
        q_ref, k_ref, v_ref, kc_ref, vc_ref, o_ref = refs
    else:
        q_ref, k_ref, v_ref, o_ref = refs
    group = HD // HKV_D
    for n in range(HKV_D):
        sk = slice(n * DH_D, (n + 1) * DH_D)
        k = k_ref[:, sk].astype(BF16)
        v = v_ref[:, sk].astype(BF16)
        if ctx:
            kc = kc_ref[:, sk].astype(BF16)
            vc = vc_ref[:, sk].astype(BF16)
        for gi in range(group):
            h = n * group + gi
            sq = slice(h * DH_D, (h + 1) * DH_D)
            q = q_ref[:, sq]
            scores = [_dot_nt(q, k) * SCALE_D]
            values = [v]
            if ctx:
                scores.append(_dot_nt(q, kc) * SCALE_D)
                values.append(vc)
            o_ref[:, sq] = _softmax_pv(scores, values).astype(o_ref.dtype)


def _gqa(dq, dk, dv, *, batch, seq, cache=None, o=None):
    tq = ATTN_TQ
    nt = seq // tq
    nkv = HKV_D * DH_D
    qrow = lambda b, i: (b * nt + i, 0)
    brow = lambda b, i: (b, 0)
    specs = [pl.BlockSpec((tq, 512), qrow), pl.BlockSpec((seq, nkv), brow), pl.BlockSpec((seq, nkv), brow)]
    args = [dq, dk, dv]
    if cache is not None:
        cspec = pl.BlockSpec((None, None, PAST_LEN, nkv), lambda b, i: (b, o, 0, 0))
        specs += [cspec, cspec]
        args += list(cache)
    return pl.pallas_call(
        functools.partial(_gqa_kernel, ctx=cache is not None),
        grid=(batch, nt),
        in_specs=specs,
        out_specs=pl.BlockSpec((tq, 512), qrow),
        out_shape=jax.ShapeDtypeStruct((batch * seq, 512), BF16),
        compiler_params=_params(("arbitrary", "arbitrary"), 32),
        name="gqa_ctx" if cache is not None else "gqa",
    )(*args)


def _mlstm_kernel(qf_ref, qb_ref, kf_ref, kb_ref, ktf_ref, ktb_ref, vf_ref, vb_ref, gcf_ref, gcb_ref, grf_ref, grb_ref,
                  bc_ref, br_ref, c0_ref, n0_ref, m0_ref,
                  hf_ref, hb_ref, cout_ref, nout_ref, mout_ref, c_s, n_s, m_s):
    c = pl.program_id(1)

    @pl.when(c == 0)
    def _():
        c_s[...] = c0_ref[...]
        n_s[...] = n0_ref[...]
        m_s[...] = m0_ref[...]

    li = lax.broadcasted_iota(jnp.int32, (CHUNK, CHUNK), 0)
    si = lax.broadcasted_iota(jnp.int32, (CHUNK, CHUNK), 1)
    hi_prec = lax.Precision.HIGHEST
    dirs = ((qf_ref, kf_ref, ktf_ref, vf_ref, gcf_ref, grf_ref, hf_ref),
            (qb_ref, kb_ref, ktb_ref, vb_ref, gcb_ref, grb_ref, hb_ref))
    for d, (q_ref, k_ref, kt_ref, v_ref, gc_ref, gr_ref, h_ref) in enumerate(dirs):
        mask = (si <= li) if d == 0 else (si >= li)
        tri = mask.astype(F32)
        gcol = gc_ref[...] + bc_ref[...]
        grow = gr_ref[...] + br_ref[...]
        g0 = 2 * HC * d
        i_col = gcol[:, g0:g0 + HC]
        lf_col = _log_sigmoid(gcol[:, g0 + HC:g0 + 2 * HC])
        i_row = grow[g0:g0 + HC, :]
        lf_row = _log_sigmoid(grow[g0 + HC:g0 + 2 * HC, :])
        b_col = jnp.dot(tri, lf_col, precision=hi_prec, preferred_element_type=F32)
        b_row = lax.dot_general(lf_row, tri, (((1,), (1,)), ((), ())), precision=hi_prec,
                                preferred_element_type=F32)
        last = CHUNK - 1 if d == 0 else 0
        for hh in range(HC):
            u = HC * d + hh
            bc = b_col[:, hh:hh + 1]
            br = b_row[hh:hh + 1, :]
            m_prev = m_s[u:u + 1, :]
            log_d = jnp.where(mask, bc - br + i_row[hh:hh + 1, :], -jnp.inf)
            inter = bc + m_prev
            m_t = jnp.maximum(inter, jnp.max(log_d, axis=1, keepdims=True))
            q = q_ref[:, hh * DQK_C:(hh + 1) * DQK_C]
            kt = kt_ref[hh * DQK_C:(hh + 1) * DQK_C, :]
            v = v_ref[:, hh * DV_C:(hh + 1) * DV_C]
            s = _dot(q, kt) * jnp.exp(log_d - m_t)
            w0 = jnp.exp(inter - m_t)
            ct = c_s[u]
            n_row = n_s[u:u + 1, :]
            num = _dot(s.astype(BF16), v) + w0 * _dot(q, ct.astype(BF16))
            den = jnp.sum(s, axis=1, keepdims=True) + w0 * jnp.sum(q.astype(F32) * n_row, axis=1, keepdims=True)
            h_ref[:, hh * DV_C:(hh + 1) * DV_C] = num / jnp.maximum(jnp.abs(den), jnp.exp(-m_t))
            b_last = bc[last:last + 1, :]
            g = b_last - bc + i_col[:, hh:hh + 1]
            m_new = jnp.maximum(b_last + m_prev, jnp.max(g, axis=0, keepdims=True))
            wg = jnp.exp(g - m_new)
            decay = jnp.exp(b_last + m_prev - m_new)
            c_s[u] = decay * ct + _dot(kt, (wg * v.astype(F32)).astype(BF16))
            k = k_ref[:, hh * DQK_C:(hh + 1) * DQK_C].astype(F32)
            n_s[u:u + 1, :] = decay * n_row + jnp.sum(wg * k, axis=0, keepdims=True)
            m_s[u:u + 1, :] = m_new

    @pl.when(c == pl.num_programs(1) - 1)
    def _():
        cout_ref[...] = c_s[...]
        nout_ref[...] = n_s[...]
        mout_ref[...] = m_s[...]


def _mlstm(cq, ck, kt, cv, gcol, grow, bias_col, bias_row, c0t, n0, m0, *, batch, seq):
    nc = seq // CHUNK
    nu = 2 * HC
    fwd = lambda b, c: (b * nc + c, 0)
    bwd = lambda b, c: (b * nc + nc - 1 - c, 0)
    fwd4 = lambda b, c: (b, c, 0, 0)
    bwd4 = lambda b, c: (b, nc - 1 - c, 0, 0)
    st3 = lambda b, c: (b, 0, 0)
    st4 = lambda b, c: (b, 0, 0, 0)
    nqk = HC * DQK_C
    nv = HC * DV_C
    ng = 4 * HC
    in_specs = [pl.BlockSpec((CHUNK, nqk), fwd), pl.BlockSpec((CHUNK, nqk), bwd),
                pl.BlockSpec((CHUNK, nqk), fwd), pl.BlockSpec((CHUNK, nqk), bwd),
                pl.BlockSpec((None, None, nqk, CHUNK), fwd4), pl.BlockSpec((None, None, nqk, CHUNK), bwd4),
                pl.BlockSpec((CHUNK, nv), fwd), pl.BlockSpec((CHUNK, nv), bwd),
                pl.BlockSpec((CHUNK, ng), fwd), pl.BlockSpec((CHUNK, ng), bwd),
                pl.BlockSpec((None, None, ng, CHUNK), fwd4), pl.BlockSpec((None, None, ng, CHUNK), bwd4),
                pl.BlockSpec((1, ng), lambda b, c: (0, 0)), pl.BlockSpec((ng, 1), lambda b, c: (0, 0)),
                pl.BlockSpec((None, nu, DQK_C, DV_C), st4), pl.BlockSpec((None, nu, DQK_C), st3),
                pl.BlockSpec((None, nu, 1), st3)]
    out_specs = [pl.BlockSpec((CHUNK, nv), fwd), pl.BlockSpec((CHUNK, nv), bwd),
                 pl.BlockSpec((None, nu, DQK_C, DV_C), st4), pl.BlockSpec((None, nu, DQK_C), st3),
                 pl.BlockSpec((None, nu, 1), st3)]
    out_shape = [jax.ShapeDtypeStruct((batch * seq, nv), F32), jax.ShapeDtypeStruct((batch * seq, nv), F32),
                 jax.ShapeDtypeStruct((batch, nu, DQK_C, DV_C), F32), jax.ShapeDtypeStruct((batch, nu, DQK_C), F32),
                 jax.ShapeDtypeStruct((batch, nu, 1), F32)]
    return pl.pallas_call(
        _mlstm_kernel,
        grid=(batch, nc),
        in_specs=in_specs,
        out_specs=out_specs,
        out_shape=out_shape,
        scratch_shapes=[pltpu.VMEM((nu, DQK_C, DV_C), F32), pltpu.VMEM((nu, DQK_C), F32), pltpu.VMEM((nu, 1), F32)],
        compiler_params=_params(("arbitrary", "arbitrary"), 32),
        name="mlstm",
    )(cq, cq, ck, ck, kt, kt, cv, cv, gcol, gcol, grow, grow, bias_col, bias_row, c0t, n0, m0)


def _axial_perm(n):
    d = n // 2
    half = d // 2
    j = np.arange(n)
    return np.where((j % d) < half, j + half, j - half)


def _axial_tables(n):
    d = n // 2
    half = d // 2
    t = jnp.arange(DEC_SEQ)
    freqs = ROPE_THETA ** (-jnp.arange(half, dtype=F32) / half)

    def cs(pos):
        ang = pos.astype(F32)[:, None] * freqs[None, :]
        return jnp.cos(ang), jnp.sin(ang)

    cr, sr = cs(t // GRID_W)
    cc, sc = cs(t % GRID_W)
    cos = jnp.concatenate([cr, cr, cc, cc], axis=-1)
    sin = jnp.concatenate([-sr, sr, -sc, sc], axis=-1)
    return cos, sin


def _nbr_bias_table(a_rpb):
    col = jnp.arange(GRID_W)
    cs = jnp.clip(col - WIN_C // 2, 0, GRID_W - WIN_C)
    col_ok = (col[None, :] >= cs[:, None]) & (col[None, :] < cs[:, None] + WIN_C)
    co_idx = jnp.clip(col[None, :] - col[:, None], 1 - WIN_C, WIN_C - 1) + (WIN_C - 1)
    dlt = jnp.arange(WIN_R)
    ro_idx = jnp.arange(WIN_R)[None, :] - dlt[:, None] + (WIN_R - 1)
    bias = a_rpb[:, :, ro_idx[:, None, :, None], co_idx[None, :, None, :]]
    bias = jnp.where(col_ok[None, None, None, :, None, :], bias, -jnp.inf)
    return bias.reshape(N_EVEN, HA, WIN_R, GRID_W, WIN_R * GRID_W)


def _head_major(n_heads, widths):
    per = sum(widths)
    cols = []
    off = 0
    for wd in widths:
        for h in range(n_heads):
            cols.extend(range(h * per + off, h * per + off + wd))
        off += wd
    return cols


def kernel(x_prompt, x_sample, cache_a_k, cache_a_v, cache_b_ckv, cache_b_krope, cache_d_k, cache_d_v, state_c_C, state_c_n, state_c_m, c, c_ctx, w_mod, b_mod, norm_g, ffn_in, ffn_out, w_in_even, w_in_odd, w_out, a_rpb, b_q_norm, b_wq_up, b_kv_norm, b_wkv_up, c_gate_bias, c_out_norm, d_q_norm, d_k_norm, final_norm):
    RP = BATCH * SEQ
    RS = DEC_BATCH * DEC_SEQ
    xp = x_prompt.reshape(RP, D_MODEL)
    xs = x_sample.reshape(RS, D_MODEL)

    cond = jnp.concatenate([c_ctx[None, :], c, jnp.zeros((N_COND - 1 - DEC_BATCH, D_MODEL), F32)], axis=0)
    mod = _modulation(cond, w_mod, b_mod)

    def groups(tm):
        per = DEC_SEQ // tm
        return (lambda i: 0), (lambda i: 1 + i // per)

    cos_b, sin_b = _axial_tables(ROPE_B)
    perm_b = _axial_perm(ROPE_B)
    q_cols = np.asarray(_head_major(HB, [NOPE_B, ROPE_B]))
    rope_cols = q_cols[HB * NOPE_B:].reshape(HB, ROPE_B)
    wq_p = jnp.take(b_wq_up, np.concatenate([q_cols, rope_cols[:, perm_b].reshape(-1)]), axis=2)
    wkv_p = jnp.take(b_wkv_up, np.asarray(_head_major(HB, [NOPE_B, VDIM_B])), axis=2)
    kr0 = 3 * HA * DH_A + Q_RANK + KV_RANK
    w_krsw = jnp.take(w_in_even, kr0 + perm_b, axis=2)
    cos_bq = jnp.tile(cos_b, (1, HB))
    sin_bq = jnp.tile(sin_b, (1, HB))
    nbr_bias = _nbr_bias_table(a_rpb)
    cache_a_k2 = cache_a_k.reshape(DEC_BATCH, N_EVEN, PAST_LEN, HA * DH_A)
    cache_a_v2 = cache_a_v.reshape(DEC_BATCH, N_EVEN, PAST_LEN, HA * DH_A)

    cos_d, sin_d = _axial_tables(DH_D)
    perm_d = _axial_perm(DH_D)
    n1 = 2 * HC * DQK_C + 2 * HC * DV_C
    ng = 4 * HC
    nq = HD * DH_D
    nk = HKV_D * DH_D
    w1 = w_in_odd[:, :, 0:n1]
    wg = w_in_odd[:, :, n1:n1 + ng]
    d0 = n1 + ng
    wd = w_in_odd[:, :, d0:d0 + nq + 2 * nk]
    sw_q = (np.arange(HD)[:, None] * DH_D + perm_d[None, :]).reshape(-1)
    sw_k = (np.arange(HKV_D)[:, None] * DH_D + perm_d[None, :]).reshape(-1)
    w_dsw = jnp.take(w_in_odd, np.concatenate([d0 + sw_q, d0 + nq + sw_k]), axis=2)
    qn_t = jnp.tile(d_q_norm, (1, HD)).reshape(N_ODD, 1, nq)
    kn_t = jnp.tile(d_k_norm, (1, HKV_D)).reshape(N_ODD, 1, nk)
    qnsw_t = jnp.tile(d_q_norm[:, perm_d], (1, HD)).reshape(N_ODD, 1, nq)
    knsw_t = jnp.tile(d_k_norm[:, perm_d], (1, HKV_D)).reshape(N_ODD, 1, nk)
    cos_dq = jnp.tile(cos_d, (1, HD))
    sin_dq = jnp.tile(sin_d, (1, HD))
    cos_dk = jnp.tile(cos_d, (1, HKV_D))
    sin_dk = jnp.tile(sin_d, (1, HKV_D))
    cache_d_k2 = cache_d_k.reshape(DEC_BATCH, N_ODD, PAST_LEN, nk)
    cache_d_v2 = cache_d_v.reshape(DEC_BATCH, N_ODD, PAST_LEN, nk)
    nu = 2 * HC
    gate_bias_col = c_gate_bias.reshape(N_ODD, 1, ng)
    gate_bias_row = c_gate_bias.reshape(N_ODD, ng, 1)

    def chunk_major_t(a, batch, seq):
        n = a.shape[1]
        return a.reshape(batch, seq // CHUNK, CHUNK, n).swapaxes(2, 3)

    even_new = []
    odd_new = []
    for l in range(DEPTH):
        gp_f, gs_f = groups(FFN_TM)
        xp = _ffn(xp, mod, norm_g, ffn_in, ffn_out, final_norm, l=l, k=0, group_of_tile=gp_f, final=False)
        xs = _ffn(xs, mod, norm_g, ffn_in, ffn_out, final_norm, l=l, k=0, group_of_tile=gs_f, final=False)
        gp_p, gs_p = groups(PROJ_TM)
        gp_o, gs_o = groups(OPROJ_TM)
        if l % 2 == 0:
            e = l // 2
            aq, ak, av, qn, qr, ckv, kr, kn, vb = _proj_even(
                xp, mod, norm_g, w_in_even, b_q_norm, wq_p, b_kv_norm, wkv_p, l=l, e=e, group_of_tile=gp_p)
            oa, ob = _attn_even_prompt(aq, ak, av, qn, qr, kn, kr, vb)
            xp = _oproj(xp, mod, w_out, oa, ob, l=l, group_of_tile=gp_o, odd=False)
            even_new.append((ak.reshape(BATCH, SEQ, HA, DH_A), av.reshape(BATCH, SEQ, HA, DH_A),
                             ckv.reshape(BATCH, SEQ, KV_RANK), kr.reshape(BATCH, SEQ, ROPE_B)))

            aq, ak, av, qn, qr, ckv, kr, kn, vb = _proj_even(
                xs, mod, norm_g, w_in_even, b_q_norm, wq_p, b_kv_norm, wkv_p, l=l, e=e, group_of_tile=gs_p,
                rope_args=(w_krsw, cos_bq, sin_bq, cos_b, sin_b))
            oa = _nbr_attention(aq, ak, av, cache_a_k2, cache_a_v2, nbr_bias, e=e)
            ob = _mla_sample(qn, qr, kn, kr, vb, cache_b_ckv, cache_b_krope, wkv_p, e=e)
            xs = _oproj(xs, mod, w_out, oa, ob, l=l, group_of_tile=gs_o, odd=False)
        else:
            o = l // 2
            cq, ck, cv, co, gt, dq, dk, dv = _proj_odd(
                xp, mod, norm_g, w1, wg, wd, qn_t, kn_t, l=l, o=o, group_of_tile=gp_p)
            od = _gqa(dq, dk, dv, batch=BATCH, seq=SEQ)
            hf, hb, ct, n_new, m_new = _mlstm(
                cq, ck, chunk_major_t(ck, BATCH, SEQ), cv, gt, chunk_major_t(gt, BATCH, SEQ),
                gate_bias_col[o], gate_bias_row[o],
                jnp.zeros((BATCH, nu, DQK_C, DV_C), F32), jnp.zeros((BATCH, nu, DQK_C), F32),
                jnp.zeros((BATCH, nu, 1), F32), batch=BATCH, seq=SEQ)
            xp = _oproj(xp, mod, w_out, (hf, hb, co), od, l=l, group_of_tile=gp_o, odd=True, out_norm=c_out_norm[o])
            odd_new.append((dk.reshape(BATCH, SEQ, HKV_D, DH_D), dv.reshape(BATCH, SEQ, HKV_D, DH_D),
                            ct.reshape(BATCH, 2, HC, DQK_C, DV_C).swapaxes(3, 4),
                            n_new.reshape(BATCH, 2, HC, DQK_C), m_new.reshape(BATCH, 2, HC)))

            cq, ck, cv, co, gt, dq, dk, dv = _proj_odd(
                xs, mod, norm_g, w1, wg, wd, qn_t, kn_t, l=l, o=o, group_of_tile=gs_p,
                rope_args=(w_dsw, qnsw_t, knsw_t, cos_dq, sin_dq, cos_dk, sin_dk))
            od = _gqa(dq, dk, dv, batch=DEC_BATCH, seq=DEC_SEQ, cache=(cache_d_k2, cache_d_v2), o=o)
            c0t = state_c_C[:, o].reshape(DEC_BATCH, nu, DV_C, DQK_C).swapaxes(2, 3)
            hf, hb, _, _, _ = _mlstm(
                cq, ck, chunk_major_t(ck, DEC_BATCH, DEC_SEQ), cv, gt, chunk_major_t(gt, DEC_BATCH, DEC_SEQ),
                gate_bias_col[o], gate_bias_row[o],
                c0t, state_c_n[:, o].reshape(DEC_BATCH, nu, DQK_C), state_c_m[:, o].reshape(DEC_BATCH, nu, 1),
                batch=DEC_BATCH, seq=DEC_SEQ)
            xs = _oproj(xs, mod, w_out, (hf, hb, co), od, l=l, group_of_tile=gs_o, odd=True, out_norm=c_out_norm[o])
        final = l == DEPTH - 1
        xp = _ffn(xp, mod, norm_g, ffn_in, ffn_out, final_norm, l=l, k=2, group_of_tile=gp_f, final=final)
        xs = _ffn(xs, mod, norm_g, ffn_in, ffn_out, final_norm, l=l, k=2, group_of_tile=gs_f, final=final)

    y_prompt = xp.reshape(BATCH, SEQ, D_MODEL)
    y_sample = xs.reshape(DEC_BATCH, DEC_SEQ, D_MODEL)
    stack = lambda items, i: jnp.stack([t[i] for t in items], axis=1)
    return (y_prompt, y_sample,
            stack(even_new, 0), stack(even_new, 1), stack(even_new, 2), stack(even_new, 3),
            stack(odd_new, 0), stack(odd_new, 1), stack(odd_new, 2), stack(odd_new, 3), stack(odd_new, 4))
```
